```python
import math
import jax, jax.numpy as jnp
from jax import lax
import numpy as np

D_MODEL = 4096
BATCH = 2
SEQ = 4096
DEPTH = 1
DEC_BATCH = 8
DEC_SEQ = 32
PAST_LEN = 2048

CHUNK = 64
Q_BLOCK = 128
EPS = 1e-6
ROPE_THETA = 10000.0
MLA_HEADS = 16
MLA_NOPE_DIM = 128
MLA_ROPE_DIM = 64
MLA_V_DIM = 128
Q_LORA = 1024
KV_LORA = 512
MLA_WIDTH = MLA_HEADS * MLA_V_DIM
MLA_IN = Q_LORA + KV_LORA + MLA_ROPE_DIM
MLA_SCALE = (MLA_NOPE_DIM + MLA_ROPE_DIM) ** -0.5
SB_HEADS = 16
SB_HEAD_DIM = 128
SB_WIDTH = SB_HEADS * SB_HEAD_DIM
SB_SCALE = SB_HEAD_DIM ** -0.5
MIX_WIDTH = MLA_WIDTH + SB_WIDTH
IN_COLS = MLA_IN + 3 * SB_WIDTH
D_FF = -(-(8 * D_MODEL) // (3 * 256)) * 256
N_MOD = 6

kernel_name = "hybrid_mla_stickbreaking_streaming_step"


def rmsnorm(x, g):
    xf = x.astype(jnp.float32)
    y = xf * lax.rsqrt(jnp.mean(xf * xf, axis=-1, keepdims=True) + EPS)
    return (y * g.astype(jnp.float32)).astype(x.dtype)


def rope(x, pos):
    half = MLA_ROPE_DIM // 2
    inv = 1.0 / (ROPE_THETA ** (jnp.arange(half, dtype=jnp.float32) / half))
    ang = pos.astype(jnp.float32)[:, None] * inv[None, :]
    cos = jnp.cos(ang)[:, None, :].astype(x.dtype)
    sin = jnp.sin(ang)[:, None, :].astype(x.dtype)
    x1, x2 = x[..., :half], x[..., half:]
    return jnp.concatenate([x1 * cos - x2 * sin, x2 * cos + x1 * sin], axis=-1)


def adaln(c, w_ada, b_ada):
    mod = jax.nn.silu(c) @ w_ada + b_ada
    return jnp.split(mod[:, None, :], N_MOD, axis=-1)


def mixer_inputs(h, pos, w_in, g_q_lat, g_kv_lat, w_uq, w_uk):
    B, S, _ = h.shape
    proj = h @ w_in
    q_lat = proj[..., :Q_LORA]
    kv_lat = proj[..., Q_LORA:Q_LORA + KV_LORA]
    k_r = proj[..., Q_LORA + KV_LORA:MLA_IN]
    sb = proj[..., MLA_IN:].reshape(B, S, 3, SB_HEADS, SB_HEAD_DIM)
    q = jnp.einsum('bsr,rhe->bshe', rmsnorm(q_lat, g_q_lat), w_uq)
    q_abs = jnp.einsum('bshn,chn->bshc', q[..., :MLA_NOPE_DIM], w_uk)
    q_rope = rope(q[..., MLA_NOPE_DIM:], pos)
    latent = rmsnorm(kv_lat, g_kv_lat)
    k_rope = rope(k_r[:, :, None, :], pos)[:, :, 0, :]
    return q_abs, q_rope, latent, k_rope, sb[:, :, 0], sb[:, :, 1], sb[:, :, 2]


def mla_attend(q_abs, q_rope, q_pos, latent, k_rope, k_pos, w_uv):
    s = (jnp.einsum('bqhc,bkc->bhqk', q_abs, latent)
         + jnp.einsum('bqhr,bkr->bhqk', q_rope, k_rope)).astype(jnp.float32) * MLA_SCALE
    visible = (k_pos[None, :] // CHUNK) <= (q_pos[:, None] // CHUNK)
    p = jax.nn.softmax(jnp.where(visible, s, -jnp.inf), axis=-1).astype(latent.dtype)
    o_lat = jnp.einsum('bhqk,bkc->bqhc', p, latent)
    return jnp.einsum('bqhc,chv->bqhv', o_lat, w_uv)


def sb_attend(q, q_pos, k, v, k_pos):
    z = jnp.einsum('bqhd,bkhd->bhqk', q, k).astype(jnp.float32) * SB_SCALE
    before = k_pos[None, :] < q_pos[:, None]
    log_keep = jnp.where(before, jax.nn.log_sigmoid(-z), 0.0)
    tail = lax.cumsum(log_keep, axis=3, reverse=True) - log_keep
    log_a = jnp.where(before, jax.nn.log_sigmoid(z) + tail, -jnp.inf)
    a = jnp.exp(log_a).astype(v.dtype)
    return jnp.einsum('bhqk,bkhd->bqhd', a, v)


def to_blocks(x):
    B, S = x.shape[0], x.shape[1]
    return jnp.moveaxis(x.reshape((B, S // Q_BLOCK, Q_BLOCK) + x.shape[2:]), 1, 0)


def from_blocks(x):
    nb, B = x.shape[0], x.shape[1]
    x = jnp.moveaxis(x, 0, 1)
    return x.reshape((B, nb * Q_BLOCK) + x.shape[3:])


def trunk_layer(x, c, pos, past, w_ada, b_ada, g_mix, g_ffn, w_in, g_q_lat, g_kv_lat,
                w_uq, w_uk, w_uv, g_out_mla, g_out_sb, w_out, w_gate, w_up, w_down):
    B, S, _ = x.shape
    sh_m, sc_m, gt_m, sh_f, sc_f, gt_f = adaln(c, w_ada, b_ada)
    h = rmsnorm(x, g_mix) * (1.0 + sc_m) + sh_m
    q_abs, q_rope, lat, k_rope, sb_q, sb_k, sb_v = mixer_inputs(
        h, pos, w_in, g_q_lat, g_kv_lat, w_uq, w_uk)
    if past is None:
        def one_block(blk):
            qa, qr, sq, qp = blk
            return (mla_attend(qa, qr, qp, lat, k_rope, pos, w_uv),
                    sb_attend(sq, qp, sb_k, sb_v, pos))
        blocks = (to_blocks(q_abs), to_blocks(q_rope), to_blocks(sb_q),
                  pos.reshape(S // Q_BLOCK, Q_BLOCK))
        o_mla, o_sb = lax.map(one_block, blocks)
        o_mla, o_sb = from_blocks(o_mla), from_blocks(o_sb)
    else:
        c_lat, c_kr, c_k, c_v = past
        k_pos = jnp.arange(c_lat.shape[1] + S)
        o_mla = mla_attend(q_abs, q_rope, pos, jnp.concatenate([c_lat, lat], axis=1),
                           jnp.concatenate([c_kr, k_rope], axis=1), k_pos, w_uv)
        o_sb = sb_attend(sb_q, pos, jnp.concatenate([c_k, sb_k], axis=1),
                         jnp.concatenate([c_v, sb_v], axis=1), k_pos)
    merged = jnp.concatenate([rmsnorm(o_mla.reshape(B, S, MLA_WIDTH), g_out_mla),
                              rmsnorm(o_sb.reshape(B, S, SB_WIDTH), g_out_sb)], axis=-1)
    x = x + gt_m * (merged @ w_out)
    h = rmsnorm(x, g_ffn) * (1.0 + sc_f) + sh_f
    x = x + gt_f * ((jax.nn.silu(h @ w_gate) * (h @ w_up)) @ w_down)
    return x, (lat, k_rope, sb_k, sb_v)


def setup_inputs(seed: int = 0) -> dict:
    key = jax.random.key(seed)
    ks = jax.random.split(key, 32)
    f32 = jnp.float32
    nrm = lambda k, shape, scale: jax.random.normal(k, shape, f32) * scale
    gain = lambda k, shape: 1.0 + 0.02 * jax.random.normal(k, shape, f32)
    L = DEPTH
    return {
        "x_prompt": nrm(ks[0], (BATCH, SEQ, D_MODEL), 1.0),
        "x_sample": nrm(ks[1], (DEC_BATCH, DEC_SEQ, D_MODEL), 1.0),
        "cache_mla_latent": nrm(ks[2], (L, DEC_BATCH, PAST_LEN, KV_LORA), 1.0),
        "cache_mla_krope": nrm(ks[3], (L, DEC_BATCH, PAST_LEN, MLA_ROPE_DIM), 1.0),
        "cache_sb_k": nrm(ks[4], (L, DEC_BATCH, PAST_LEN, SB_HEADS, SB_HEAD_DIM), 1.0),
        "cache_sb_v": nrm(ks[5], (L, DEC_BATCH, PAST_LEN, SB_HEADS, SB_HEAD_DIM), 1.0),
        "c_prompt": nrm(ks[6], (BATCH, D_MODEL), 1.0),
        "c_sample": nrm(ks[7], (DEC_BATCH, D_MODEL), 1.0),
        "w_ada": nrm(ks[8], (L, D_MODEL, N_MOD * D_MODEL), 0.5 * D_MODEL ** -0.5),
        "b_ada": nrm(ks[9], (L, N_MOD * D_MODEL), 0.02),
        "g_mix": gain(ks[10], (L, D_MODEL)),
        "g_ffn": gain(ks[11], (L, D_MODEL)),
        "w_in": nrm(ks[12], (L, D_MODEL, IN_COLS), D_MODEL ** -0.5),
        "g_q_lat": gain(ks[13], (L, Q_LORA)),
        "g_kv_lat": gain(ks[14], (L, KV_LORA)),
        "w_uq": nrm(ks[15], (L, Q_LORA, MLA_HEADS, MLA_NOPE_DIM + MLA_ROPE_DIM), Q_LORA ** -0.5),
        "w_uk": nrm(ks[16], (L, KV_LORA, MLA_HEADS, MLA_NOPE_DIM), KV_LORA ** -0.5),
        "w_uv": nrm(ks[17], (L, KV_LORA, MLA_HEADS, MLA_V_DIM), KV_LORA ** -0.5),
        "g_out_mla": gain(ks[18], (L, MLA_WIDTH)),
        "g_out_sb": gain(ks[19], (L, SB_WIDTH)),
        "w_out": nrm(ks[20], (L, MIX_WIDTH, D_MODEL), MIX_WIDTH ** -0.5),
        "w_gate": nrm(ks[21], (L, D_MODEL, D_FF), D_MODEL ** -0.5),
        "w_up": nrm(ks[22], (L, D_MODEL, D_FF), D_MODEL ** -0.5),
        "w_down": nrm(ks[23], (L, D_FF, D_MODEL), D_FF ** -0.5),
        "g_final": gain(ks[24], (D_MODEL,)),
    }


def reference(x_prompt, x_sample, cache_mla_latent, cache_mla_krope, cache_sb_k, cache_sb_v,
              c_prompt, c_sample, w_ada, b_ada, g_mix, g_ffn, w_in, g_q_lat, g_kv_lat,
              w_uq, w_uk, w_uv, g_out_mla, g_out_sb, w_out, w_gate, w_up, w_down, g_final):
    pos_p = jnp.arange(x_prompt.shape[1])
    pos_s = cache_mla_latent.shape[2] + jnp.arange(x_sample.shape[1])
    xp, xs = x_prompt, x_sample
    new_p, new_s = [], []
    for l in range(DEPTH):
        w = (w_ada[l], b_ada[l], g_mix[l], g_ffn[l], w_in[l], g_q_lat[l], g_kv_lat[l],
             w_uq[l], w_uk[l], w_uv[l], g_out_mla[l], g_out_sb[l], w_out[l],
             w_gate[l], w_up[l], w_down[l])
        xp, st_p = trunk_layer(xp, c_prompt, pos_p, None, *w)
        past = (cache_mla_latent[l], cache_mla_krope[l], cache_sb_k[l], cache_sb_v[l])
        xs, st_s = trunk_layer(xs, c_sample, pos_s, past, *w)
        new_p.append(st_p)
        new_s.append(st_s)
    y_prompt = rmsnorm(xp, g_final)
    y_sample = rmsnorm(xs, g_final)
    p_lat = jnp.stack([s[0] for s in new_p], axis=0)
    p_krope = jnp.stack([s[1] for s in new_p], axis=0)
    p_sbk = jnp.stack([s[2] for s in new_p], axis=0)
    p_sbv = jnp.stack([s[3] for s in new_p], axis=0)
    s_lat = jnp.stack([s[0] for s in new_s], axis=0)
    s_krope = jnp.stack([s[1] for s in new_s], axis=0)
    s_sbk = jnp.stack([s[2] for s in new_s], axis=0)
    s_sbv = jnp.stack([s[3] for s in new_s], axis=0)
    return (y_prompt, y_sample, p_lat, p_krope, p_sbk, p_sbv, s_lat, s_krope, s_sbk, s_sbv)
```

```python
import functools

import jax
import jax.numpy as jnp
from jax import lax
from jax.experimental import pallas as pl
from jax.experimental.pallas import tpu as pltpu

F32 = jnp.float32
BF16 = jnp.bfloat16

CHUNK = 64
EPS = 1e-6
ROPE_THETA = 10000.0
MLA_HEADS = 16
MLA_NOPE = 128
MLA_ROPE = 64
MLA_QK = MLA_NOPE + MLA_ROPE
MLA_V = 128
Q_LORA = 1024
KV_LORA = 512
MLA_IN = Q_LORA + KV_LORA + MLA_ROPE
MLA_WIDTH = MLA_HEADS * MLA_V
MLA_SCALE = float(MLA_QK) ** -0.5
SB_HEADS = 16
SB_DIM = 128
SB_WIDTH = SB_HEADS * SB_DIM
SB_SCALE = float(SB_DIM) ** -0.5
N_MOD = 6
MOD_ROWS = 16

LANE = 128
VMEM_LIMIT = 56 * 1024 * 1024


def _cparams(n_axes, vmem=VMEM_LIMIT):
    return pltpu.CompilerParams(
        dimension_semantics=("arbitrary",) * n_axes, vmem_limit_bytes=vmem)


def _dot(a, b):
    return jnp.dot(a, b, preferred_element_type=F32)


def _dot_nt(a, b):
    return lax.dot_general(a, b, (((1,), (1,)), ((), ())), preferred_element_type=F32)


def _rms(x, g):
    ms = jnp.mean(x * x, axis=-1, keepdims=True)
    return x * lax.rsqrt(ms + EPS) * g


def _rope(t, cos2, sin2):
    half = MLA_ROPE // 2
    rot = jnp.concatenate([-t[:, half:], t[:, :half]], axis=-1)
    return t * cos2 + rot * sin2


def _row_tiles(i, n_full, tm, last_rows, body):
    @pl.when(i < n_full)
    def _():
        body(slice(0, tm), False)

    if last_rows:
        @pl.when(i == n_full)
        def _():
            body(slice(0, last_rows), True)


def _adaln_kernel(c_ref, w_ref, b_ref, o_ref):
    c = c_ref[...]
    s = (c * jax.nn.sigmoid(c)).astype(BF16)
    o_ref[...] = _dot(s, w_ref[...].astype(BF16)) + b_ref[...]


def _adaln(c_all, w_ada, b_ada):
    d, n = w_ada.shape
    tn = _largest_tile(n, 1024)
    return pl.pallas_call(
        _adaln_kernel,
        grid=(n // tn,),
        in_specs=[pl.BlockSpec((MOD_ROWS, d), lambda j: (0, 0)),
                  pl.BlockSpec((d, tn), lambda j: (0, j)),
                  pl.BlockSpec((1, tn), lambda j: (0, j))],
        out_specs=pl.BlockSpec((MOD_ROWS, tn), lambda j: (0, j)),
        out_shape=jax.ShapeDtypeStruct((MOD_ROWS, n), F32),
        compiler_params=_cparams(1),
        name="adaln",
    )(c_all, w_ada, b_ada.reshape(1, n))


def _norm_mod_kernel(xp_ref, xs_ref, mod_ref, g_ref, o_ref, *, n_full, tr, tiles_per_b,
                     nb_p, nb_s, s_len, sh_off, sc_off, d):
    i = pl.program_id(0)
    g = g_ref[...]

    @pl.when(i < n_full)
    def _():
        b = i // tiles_per_b
        sh = mod_ref[pl.ds(b, 1), sh_off:sh_off + d]
        sc = mod_ref[pl.ds(b, 1), sc_off:sc_off + d]
        o_ref[...] = (_rms(xp_ref[...], g) * (1.0 + sc) + sh).astype(o_ref.dtype)

    @pl.when(i == n_full)
    def _():
        for bs in range(nb_s):
            r = nb_p + bs
            sh = mod_ref[r:r + 1, sh_off:sh_off + d]
            sc = mod_ref[r:r + 1, sc_off:sc_off + d]
            rows = slice(bs * s_len, (bs + 1) * s_len)
            o_ref[rows, :] = (_rms(xs_ref[rows, :], g) * (1.0 + sc) + sh).astype(o_ref.dtype)


def _norm_mod(xp, xs, xs_block, mod, g, *, n_p, n_s, nb_p, nb_s, sh_off, sc_off, tr):
    d = xp.shape[1]
    n_full = n_p // tr
    kern = functools.partial(
        _norm_mod_kernel, n_full=n_full, tr=tr, tiles_per_b=(n_p // nb_p) // tr,
        nb_p=nb_p, nb_s=nb_s, s_len=n_s // nb_s, sh_off=sh_off, sc_off=sc_off, d=d)
    return pl.pallas_call(
        kern,
        grid=(n_full + 1,),
        in_specs=[pl.BlockSpec((tr, d), lambda i: (jnp.minimum(i, n_full - 1), 0)),
                  pl.BlockSpec((n_s, d), lambda i: (xs_block, 0)),
                  pl.BlockSpec(mod.shape, lambda i: (0, 0)),
                  pl.BlockSpec((1, d), lambda i: (0, 0))],
        out_specs=pl.BlockSpec((tr, d), lambda i: (i, 0)),
        out_shape=jax.ShapeDtypeStruct((n_p + n_s, d), BF16),
        compiler_params=_cparams(1),
        name="norm_mod",
    )(xp, xs, mod, g.reshape(1, d))


def _inproj_a_kernel(a_ref, w_ref, gq_ref, gkv_ref, cos_ref, sin_ref,
                     qn_ref, lat_ref, kr_ref, *, n_full, tm, last_rows):
    i = pl.program_id(0)

    def body(rows, _):
        r = _dot(a_ref[rows, :], w_ref[...])
        qn_ref[rows, :] = _rms(r[:, :Q_LORA], gq_ref[...]).astype(qn_ref.dtype)
        lat_ref[rows, :] = _rms(r[:, Q_LORA:Q_LORA + KV_LORA], gkv_ref[...])
        kr_ref[rows, :] = _rope(r[:, Q_LORA + KV_LORA:MLA_IN], cos_ref[rows, :], sin_ref[rows, :])

    _row_tiles(i, n_full, tm, last_rows, body)


def _inproj_a(h, w_a, g_q, g_kv, cos2, sin2, *, n_p, n_s, tm):
    t, d = h.shape
    n_full = n_p // tm
    na = w_a.shape[1]
    kern = functools.partial(_inproj_a_kernel, n_full=n_full, tm=tm, last_rows=n_s)
    row = lambda i: (i, 0)
    fixed = lambda i: (0, 0)
    return pl.pallas_call(
        kern,
        grid=(n_full + 1,),
        in_specs=[pl.BlockSpec((tm, d), row),
                  pl.BlockSpec((d, na), fixed),
                  pl.BlockSpec((1, Q_LORA), fixed),
                  pl.BlockSpec((1, KV_LORA), fixed),
                  pl.BlockSpec((tm, MLA_ROPE), row),
                  pl.BlockSpec((tm, MLA_ROPE), row)],
        out_specs=[pl.BlockSpec((tm, Q_LORA), row),
                   pl.BlockSpec((tm, KV_LORA), row),
                   pl.BlockSpec((tm, MLA_ROPE), row)],
        out_shape=[jax.ShapeDtypeStruct((t, Q_LORA), BF16),
                   jax.ShapeDtypeStruct((t, KV_LORA), F32),
                   jax.ShapeDtypeStruct((t, MLA_ROPE), F32)],
        compiler_params=_cparams(1),
        name="inproj_a",
    )(h, w_a, g_q.reshape(1, -1), g_kv.reshape(1, -1), cos2, sin2)


def _mm_kernel(a_ref, w_ref, o_ref, *, n_full, tm, last_rows, scale):
    i = pl.program_id(0)
    w = w_ref[...].astype(BF16)

    def body(rows, _):
        r = _dot(a_ref[rows, :], w)
        if scale is not None:
            r = r * scale
        o_ref[rows, :] = r.astype(o_ref.dtype)

    _row_tiles(i, n_full, tm, last_rows, body)


def _mm(a, w, *, tm, row_block0, n_full, last_rows, tn, col_block0, n_cols, out_dtype,
        scale=None, name="mm"):
    k = a.shape[1]
    n_row_tiles = n_full + (1 if last_rows else 0)
    out_rows = n_full * tm + last_rows
    kern = functools.partial(_mm_kernel, n_full=n_full, tm=tm, last_rows=last_rows, scale=scale)
    return pl.pallas_call(
        kern,
        grid=(n_row_tiles, n_cols // tn),
        in_specs=[pl.BlockSpec((tm, k), lambda i, j: (row_block0 + i, 0)),
                  pl.BlockSpec((k, tn), lambda i, j: (0, col_block0 + j))],
        out_specs=pl.BlockSpec((tm, tn), lambda i, j: (i, j)),
        out_shape=jax.ShapeDtypeStruct((out_rows, n_cols), out_dtype),
        compiler_params=_cparams(2),
        name=name,
    )(a, w)


def _qheads_kernel(a_ref, w_ref, cos_ref, sin_ref, o_ref, *, n_full, tm, last_rows):
    i = pl.program_id(0)

    def body(rows, _):
        q = _dot(a_ref[rows, :], w_ref[0])
        o_ref[0, rows, :MLA_NOPE] = (q[:, :MLA_NOPE] * MLA_SCALE).astype(o_ref.dtype)
        qr = _rope(q[:, MLA_NOPE:], cos_ref[rows, :], sin_ref[rows, :])
        o_ref[0, rows, MLA_NOPE:] = (qr * MLA_SCALE).astype(o_ref.dtype)

    _row_tiles(i, n_full, tm, last_rows, body)


def _qheads(qn, w_uq_t, cos2, sin2, *, n_p, n_s, tm):
    t, k = qn.shape
    n_full = n_p // tm
    kern = functools.partial(_qheads_kernel, n_full=n_full, tm=tm, last_rows=n_s)
    return pl.pallas_call(
        kern,
        grid=(n_full + 1, MLA_HEADS),
        in_specs=[pl.BlockSpec((tm, k), lambda i, h: (i, 0)),
                  pl.BlockSpec((1, k, MLA_QK), lambda i, h: (h, 0, 0)),
                  pl.BlockSpec((tm, MLA_ROPE), lambda i, h: (i, 0)),
                  pl.BlockSpec((tm, MLA_ROPE), lambda i, h: (i, 0))],
        out_specs=pl.BlockSpec((1, tm, MLA_QK), lambda i, h: (h, i, 0)),
        out_shape=jax.ShapeDtypeStruct((MLA_HEADS, t, MLA_QK), BF16),
        compiler_params=_cparams(2),
        name="qheads",
    )(qn, w_uq_t, cos2, sin2)


def _kv_up_kernel(lat_ref, kr_ref, wk_ref, wv_ref, k_ref, v_ref):
    lat = lat_ref[...].astype(BF16)
    kn = _dot(lat, wk_ref[...]).astype(BF16)
    v_ref[...] = _dot(lat, wv_ref[...]).astype(BF16)
    kr = kr_ref[...].astype(BF16)
    for h in range(MLA_HEADS):
        k_ref[h, :, :MLA_NOPE] = kn[:, h * MLA_NOPE:(h + 1) * MLA_NOPE]
        k_ref[h, :, MLA_NOPE:] = kr


def _kv_up(lat, kr, wk, wv, *, n_p, tm):
    fixed = lambda i: (0, 0)
    return pl.pallas_call(
        _kv_up_kernel,
        grid=(n_p // tm,),
        in_specs=[pl.BlockSpec((tm, KV_LORA), lambda i: (i, 0)),
                  pl.BlockSpec((tm, MLA_ROPE), lambda i: (i, 0)),
                  pl.BlockSpec(wk.shape, fixed),
                  pl.BlockSpec(wv.shape, fixed)],
        out_specs=[pl.BlockSpec((MLA_HEADS, tm, MLA_QK), lambda i: (0, i, 0)),
                   pl.BlockSpec((tm, MLA_WIDTH), lambda i: (i, 0))],
        out_shape=[jax.ShapeDtypeStruct((MLA_HEADS, n_p, MLA_QK), BF16),
                   jax.ShapeDtypeStruct((n_p, MLA_WIDTH), BF16)],
        compiler_params=_cparams(1),
        name="kv_up",
    )(lat, kr, wk, wv)


def _mla_prompt_kernel(q_ref, k_ref, v_ref, o_ref, m_sc, l_sc, acc_sc, *, tq):
    qi = pl.program_id(2)
    q = q_ref[0]
    m_sc[...] = jnp.full(m_sc.shape, -jnp.inf, F32)
    l_sc[...] = jnp.zeros(l_sc.shape, F32)
    acc_sc[...] = jnp.zeros(acc_sc.shape, F32)

    def step(j, diag):
        start = pl.multiple_of(j * tq, tq)
        k = k_ref[0, pl.ds(start, tq), :]
        v = v_ref[pl.ds(start, tq), :]
        s = _dot_nt(q, k)
        if diag:
            row = lax.broadcasted_iota(jnp.int32, s.shape, 0) // CHUNK
            col = lax.broadcasted_iota(jnp.int32, s.shape, 1) // CHUNK
            s = jnp.where(col <= row, s, -jnp.inf)
        m_prev = m_sc[...]
        m_new = jnp.maximum(m_prev, jnp.max(s, axis=-1, keepdims=True))
        alpha = jnp.exp(m_prev - m_new)
        p = jnp.exp(s - m_new)
        l_sc[...] = alpha * l_sc[...] + jnp.sum(p, axis=-1, keepdims=True)
        acc_sc[...] = alpha * acc_sc[...] + _dot(p.astype(BF16), v)
        m_sc[...] = m_new

    def loop_body(j, c):
        step(j, False)
        return c

    lax.fori_loop(0, qi, loop_body, 0)
    step(qi, True)
    o_ref[...] = acc_sc[...] / l_sc[...]


def _mla_prompt(q, kcat, v, o_rows, *, nb, s_len, tq):
    nq = s_len // tq
    kern = functools.partial(_mla_prompt_kernel, tq=tq)
    return pl.pallas_call(
        kern,
        grid=(nb, MLA_HEADS, nq),
        in_specs=[pl.BlockSpec((1, tq, MLA_QK), lambda b, h, i: (h, b * nq + i, 0)),
                  pl.BlockSpec((1, s_len, MLA_QK), lambda b, h, i: (h, b, 0)),
                  pl.BlockSpec((s_len, MLA_V), lambda b, h, i: (b, h))],
        out_specs=pl.BlockSpec((tq, MLA_V), lambda b, h, i: (b * nq + i, h)),
        out_shape=jax.ShapeDtypeStruct((o_rows, MLA_WIDTH), F32),
        scratch_shapes=[pltpu.VMEM((tq, 1), F32), pltpu.VMEM((tq, 1), F32),
                        pltpu.VMEM((tq, MLA_V), F32)],
        compiler_params=_cparams(3),
        name="mla_prompt",
    )(q, kcat, v)


def _sb_tile(q, k, v, carry, mask, tri):
    z = _dot_nt(q, k)
    lk = -(jnp.maximum(z, 0.0) + jnp.log1p(jnp.exp(-jnp.abs(z))))
    if mask is not None:
        lk = jnp.where(mask, lk, 0.0)
    hi = lk.astype(BF16)
    lo = (lk - hi.astype(F32)).astype(BF16)
    csum = _dot(hi, tri) + _dot(lo, tri)
    a = jnp.exp(z + csum + carry)
    if mask is not None:
        a = jnp.where(mask, a, 0.0)
    return _dot(a.astype(BF16), v), carry + csum[:, 0:1]


def _tri(n):
    r = lax.broadcasted_iota(jnp.int32, (n, n), 0)
    c = lax.broadcasted_iota(jnp.int32, (n, n), 1)
    return (r >= c).astype(BF16)


def _sb_prompt_kernel(q_ref, k_ref, v_ref, o_ref, carry_sc, acc_sc, *, tq):
    qi = pl.program_id(2)
    q = q_ref[...]
    tri = _tri(tq)
    row = lax.broadcasted_iota(jnp.int32, (tq, tq), 0)
    col = lax.broadcasted_iota(jnp.int32, (tq, tq), 1)

    def tile(j):
        start = pl.multiple_of(j * tq, tq)
        return (k_ref[pl.ds(start, tq), :].astype(BF16), v_ref[pl.ds(start, tq), :].astype(BF16))

    k, v = tile(qi)
    o, c = _sb_tile(q, k, v, jnp.zeros((tq, 1), F32), col < row, tri)
    acc_sc[...] = o
    carry_sc[...] = c

    def loop_body(jj, c0):
        k, v = tile(qi - 1 - jj)
        o, c = _sb_tile(q, k, v, carry_sc[...], None, tri)
        acc_sc[...] += o
        carry_sc[...] = c
        return c0

    lax.fori_loop(0, qi, loop_body, 0)
    o_ref[...] = acc_sc[...]


def _sb_prompt(q, k, v, o_rows, *, nb, s_len, tq):
    nq = s_len // tq
    kern = functools.partial(_sb_prompt_kernel, tq=tq)
    return pl.pallas_call(
        kern,
        grid=(nb, SB_HEADS, nq),
        in_specs=[pl.BlockSpec((tq, SB_DIM), lambda b, h, i: (b * nq + i, h)),
                  pl.BlockSpec((s_len, SB_DIM), lambda b, h, i: (b, h)),
                  pl.BlockSpec((s_len, SB_DIM), lambda b, h, i: (b, h))],
        out_specs=pl.BlockSpec((tq, SB_DIM), lambda b, h, i: (b * nq + i, h)),
        out_shape=jax.ShapeDtypeStruct((o_rows, SB_WIDTH), F32),
        scratch_shapes=[pltpu.VMEM((tq, 1), F32), pltpu.VMEM((tq, SB_DIM), F32)],
        compiler_params=_cparams(3),
        name="sb_prompt",
    )(q, k, v)


def _qabs_kernel(q_ref, w_ref, o_ref):
    o_ref[0] = _dot(q_ref[0][:, :MLA_NOPE], w_ref[0]).astype(o_ref.dtype)


def _qabs(q, w_uk_t, *, n_p, n_s):
    return pl.pallas_call(
        _qabs_kernel,
        grid=(MLA_HEADS,),
        in_specs=[pl.BlockSpec((1, n_s, MLA_QK), lambda h: (h, n_p // n_s, 0)),
                  pl.BlockSpec((1, MLA_NOPE, KV_LORA), lambda h: (h, 0, 0))],
        out_specs=pl.BlockSpec((1, n_s, KV_LORA), lambda h: (h, 0, 0)),
        out_shape=jax.ShapeDtypeStruct((MLA_HEADS, n_s, KV_LORA), BF16),
        compiler_params=_cparams(1),
        name="qabs",
    )(q, w_uk_t)


def _mla_sample_kernel(qa_ref, q_ref, clat_ref, ckr_ref, nlat_ref, nkr_ref, wv_ref, o_ref,
                       *, past, s_len):
    hm = MLA_HEADS * s_len
    qa = qa_ref[...].reshape(hm, KV_LORA)
    qr = q_ref[...][:, :, MLA_NOPE:].reshape(hm, MLA_ROPE)
    clat = clat_ref[0].astype(BF16)
    ckr = ckr_ref[0].astype(BF16)
    nlat = nlat_ref[...].astype(BF16)
    nkr = nkr_ref[...].astype(BF16)
    s_c = _dot_nt(qa, clat) + _dot_nt(qr, ckr)
    s_n = _dot_nt(qa, nlat) + _dot_nt(qr, nkr)
    t = lax.broadcasted_iota(jnp.int32, s_n.shape, 0) % s_len
    s = lax.broadcasted_iota(jnp.int32, s_n.shape, 1)
    s_n = jnp.where((past + s) // CHUNK <= (past + t) // CHUNK, s_n, -jnp.inf)
    m = jnp.maximum(jnp.max(s_c, axis=-1, keepdims=True), jnp.max(s_n, axis=-1, keepdims=True))
    p_c = jnp.exp(s_c - m)
    p_n = jnp.exp(s_n - m)
    l = jnp.sum(p_c, axis=-1, keepdims=True) + jnp.sum(p_n, axis=-1, keepdims=True)
    o_lat = (_dot(p_c.astype(BF16), clat) + _dot(p_n.astype(BF16), nlat)) / l
    o_lat = o_lat.astype(BF16)
    for h in range(MLA_HEADS):
        o_ref[:, h * MLA_V:(h + 1) * MLA_V] = _dot(
            o_lat[h * s_len:(h + 1) * s_len, :], wv_ref[:, h * MLA_V:(h + 1) * MLA_V])


def _mla_sample(qabs, q, c_lat, c_kr, lat, kr, wv, *, n_p, nb, s_len):
    past = c_lat.shape[1]
    rb0 = n_p // s_len
    kern = functools.partial(_mla_sample_kernel, past=past, s_len=s_len)
    return pl.pallas_call(
        kern,
        grid=(nb,),
        in_specs=[pl.BlockSpec((MLA_HEADS, s_len, KV_LORA), lambda b: (0, b, 0)),
                  pl.BlockSpec((MLA_HEADS, s_len, MLA_QK), lambda b: (0, rb0 + b, 0)),
                  pl.BlockSpec((1, past, KV_LORA), lambda b: (b, 0, 0)),
                  pl.BlockSpec((1, past, MLA_ROPE), lambda b: (b, 0, 0)),
                  pl.BlockSpec((s_len, KV_LORA), lambda b: (rb0 + b, 0)),
                  pl.BlockSpec((s_len, MLA_ROPE), lambda b: (rb0 + b, 0)),
                  pl.BlockSpec(wv.shape, lambda b: (0, 0))],
        out_specs=pl.BlockSpec((s_len, MLA_WIDTH), lambda b: (b, 0)),
        out_shape=jax.ShapeDtypeStruct((nb * s_len, MLA_WIDTH), F32),
        compiler_params=_cparams(1),
        name="mla_sample",
    )(qabs, q, c_lat, c_kr, lat, kr, wv)


def _sb_sample_kernel(q_ref, nk_ref, nv_ref, ck_ref, cv_ref, o_ref, carry_sc, acc_sc,
                      *, s_len, tk):
    j = pl.program_id(1)
    nj = pl.num_programs(1)
    tri = _tri(tk)

    @pl.when(j == 0)
    def _():
        tri_n = _tri(s_len)
        row = lax.broadcasted_iota(jnp.int32, (s_len, s_len), 0)
        col = lax.broadcasted_iota(jnp.int32, (s_len, s_len), 1)
        for h in range(SB_HEADS):
            cols = slice(h * SB_DIM, (h + 1) * SB_DIM)
            o, c = _sb_tile(q_ref[:, cols], nk_ref[:, cols].astype(BF16),
                            nv_ref[:, cols].astype(BF16),
                            jnp.zeros((s_len, 1), F32), col < row, tri_n)
            acc_sc[:, cols] = o
            carry_sc[:, h:h + 1] = c

    for h in range(SB_HEADS):
        cols = slice(h * SB_DIM, (h + 1) * SB_DIM)
        o, c = _sb_tile(q_ref[:, cols], ck_ref[0, :, cols].astype(BF16),
                        cv_ref[0, :, cols].astype(BF16), carry_sc[:, h:h + 1], None, tri)
        acc_sc[:, cols] += o
        carry_sc[:, h:h + 1] = c

    @pl.when(j == nj - 1)
    def _():
        o_ref[...] = acc_sc[...]


def _sb_sample(q, nk, nv, ck, cv, *, n_p, nb, s_len, tk):
    past = ck.shape[1]
    nkt = past // tk
    rb0 = n_p // s_len
    kern = functools.partial(_sb_sample_kernel, s_len=s_len, tk=tk)
    return pl.pallas_call(
        kern,
        grid=(nb, nkt),
        in_specs=[pl.BlockSpec((s_len, SB_WIDTH), lambda b, j: (rb0 + b, 0)),
                  pl.BlockSpec((s_len, SB_WIDTH), lambda b, j: (b, 0)),
                  pl.BlockSpec((s_len, SB_WIDTH), lambda b, j: (b, 0)),
                  pl.BlockSpec((1, tk, SB_WIDTH), lambda b, j: (b, nkt - 1 - j, 0)),
                  pl.BlockSpec((1, tk, SB_WIDTH), lambda b, j: (b, nkt - 1 - j, 0))],
        out_specs=pl.BlockSpec((s_len, SB_WIDTH), lambda b, j: (b, 0)),
        out_shape=jax.ShapeDtypeStruct((nb * s_len, SB_WIDTH), F32),
        scratch_shapes=[pltpu.VMEM((s_len, LANE), F32), pltpu.VMEM((s_len, SB_WIDTH), F32)],
        compiler_params=_cparams(2),
        name="sb_sample",
    )(q, nk, nv, ck, cv)


def _merge_kernel(op_mla_ref, op_sb_ref, os_mla_ref, os_sb_ref, ga_ref, gb_ref, o_ref,
                  *, n_full, n_s):
    i = pl.program_id(0)

    def put(rows, a, b):
        o_ref[rows, :MLA_WIDTH] = _rms(a, ga_ref[...]).astype(o_ref.dtype)
        o_ref[rows, MLA_WIDTH:] = _rms(b, gb_ref[...]).astype(o_ref.dtype)

    @pl.when(i < n_full)
    def _():
        put(slice(None), op_mla_ref[...], op_sb_ref[...])

    @pl.when(i == n_full)
    def _():
        put(slice(0, n_s), os_mla_ref[...], os_sb_ref[...])


def _merge(op_mla, op_sb, os_mla, os_sb, g_a, g_b, *, n_p, n_s, tr):
    n_full = n_p // tr
    kern = functools.partial(_merge_kernel, n_full=n_full, n_s=n_s)
    prow = lambda i: (jnp.minimum(i, n_full - 1), 0)
    fixed = lambda i: (0, 0)
    return pl.pallas_call(
        kern,
        grid=(n_full + 1,),
        in_specs=[pl.BlockSpec((tr, MLA_WIDTH), prow),
                  pl.BlockSpec((tr, SB_WIDTH), prow),
                  pl.BlockSpec((n_s, MLA_WIDTH), fixed),
                  pl.BlockSpec((n_s, SB_WIDTH), fixed),
                  pl.BlockSpec((1, MLA_WIDTH), fixed),
                  pl.BlockSpec((1, SB_WIDTH), fixed)],
        out_specs=pl.BlockSpec((tr, MLA_WIDTH + SB_WIDTH), lambda i: (i, 0)),
        out_shape=jax.ShapeDtypeStruct((n_p + n_s, MLA_WIDTH + SB_WIDTH), BF16),
        compiler_params=_cparams(1),
        name="merge_norm",
    )(op_mla, op_sb, os_mla, os_sb, g_a.reshape(1, -1), g_b.reshape(1, -1))


def _gated_rows(o_ref, xp_ref, xs_ref, mod_ref, r, rows, is_sample, i, *, tiles_per_b,
                nb_p, nb_s, s_len):
    if not is_sample:
        b = i // tiles_per_b
        o_ref[...] = xp_ref[...] + mod_ref[pl.ds(b, 1), :] * r
    else:
        for bs in range(nb_s):
            rr = slice(bs * s_len, (bs + 1) * s_len)
            g = mod_ref[nb_p + bs:nb_p + bs + 1, :]
            o_ref[rr, :] = xs_ref[rr, :] + g * r[rr, :]


def _mm_res_kernel(a_ref, w_ref, xp_ref, xs_ref, mod_ref, o_ref, *, n_full, tm, last_rows, **kw):
    i = pl.program_id(0)
    w = w_ref[...].astype(BF16)

    def body(rows, is_sample):
        r = _dot(a_ref[rows, :], w)
        _gated_rows(o_ref, xp_ref, xs_ref, mod_ref, r, rows, is_sample, i, **kw)

    _row_tiles(i, n_full, tm, last_rows, body)


def _mm_res(a, w, xp, xs, xs_block, mod, gate_off, *, n_p, n_s, nb_p, nb_s, tm, tn):
    k = a.shape[1]
    n = w.shape[1]
    n_full = n_p // tm
    kern = functools.partial(
        _mm_res_kernel, n_full=n_full, tm=tm, last_rows=n_s,
        tiles_per_b=(n_p // nb_p) // tm, nb_p=nb_p, nb_s=nb_s, s_len=n_s // nb_s)
    return pl.pallas_call(
        kern,
        grid=(n_full + 1, n // tn),
        in_specs=[pl.BlockSpec((tm, k), lambda i, j: (i, 0)),
                  pl.BlockSpec((k, tn), lambda i, j: (0, j)),
                  pl.BlockSpec((tm, tn), lambda i, j: (jnp.minimum(i, n_full - 1), j)),
                  pl.BlockSpec((n_s, tn), lambda i, j: (xs_block, j)),
                  pl.BlockSpec((MOD_ROWS, tn), lambda i, j: (0, gate_off // tn + j))],
        out_specs=pl.BlockSpec((tm, tn), lambda i, j: (i, j)),
        out_shape=jax.ShapeDtypeStruct((n_p + n_s, n), F32),
        compiler_params=_cparams(2),
        name="mm_res",
    )(a, w, xp, xs, mod)


def _swiglu_kernel(a_ref, wg_ref, wu_ref, o_ref, *, n_full, tm, last_rows):
    i = pl.program_id(0)
    wg = wg_ref[...].astype(BF16)
    wu = wu_ref[...].astype(BF16)

    def body(rows, _):
        a = a_ref[rows, :]
        g = _dot(a, wg)
        u = _dot(a, wu)
        o_ref[rows, :] = (g * jax.nn.sigmoid(g) * u).astype(o_ref.dtype)

    _row_tiles(i, n_full, tm, last_rows, body)


def _swiglu(a, wg, wu, *, n_p, n_s, tm, tn):
    t, k = a.shape
    n = wg.shape[1]
    n_full = n_p // tm
    kern = functools.partial(_swiglu_kernel, n_full=n_full, tm=tm, last_rows=n_s)
    return pl.pallas_call(
        kern,
        grid=(n_full + 1, n // tn),
        in_specs=[pl.BlockSpec((tm, k), lambda i, j: (i, 0)),
                  pl.BlockSpec((k, tn), lambda i, j: (0, j)),
                  pl.BlockSpec((k, tn), lambda i, j: (0, j))],
        out_specs=pl.BlockSpec((tm, tn), lambda i, j: (i, j)),
        out_shape=jax.ShapeDtypeStruct((t, n), BF16),
        compiler_params=_cparams(2),
        name="swiglu_up",
    )(a, wg, wu)


def _down_kernel(a_ref, w_ref, xp_ref, xs_ref, mod_ref, o_ref, acc_sc, *, n_full, tm,
                 last_rows, nk, **kw):
    i = pl.program_id(0)
    kk = pl.program_id(2)
    w = w_ref[...].astype(BF16)

    def body(rows, is_sample):
        r = _dot(a_ref[rows, :], w)

        @pl.when(kk == 0)
        def _():
            acc_sc[rows, :] = r

        @pl.when(jnp.logical_and(kk > 0, kk < nk - 1))
        def _():
            acc_sc[rows, :] += r

        @pl.when(kk == nk - 1)
        def _():
            tot = r if nk == 1 else acc_sc[rows, :] + r
            _gated_rows(o_ref, xp_ref, xs_ref, mod_ref, tot, rows, is_sample, i, **kw)

    _row_tiles(i, n_full, tm, last_rows, body)


def _down(a, w, x_all, mod, gate_off, *, n_p, n_s, nb_p, nb_s, tm, tn, tk):
    k = a.shape[1]
    n = w.shape[1]
    nk = k // tk
    n_full = n_p // tm
    kern = functools.partial(
        _down_kernel, n_full=n_full, tm=tm, last_rows=n_s, nk=nk,
        tiles_per_b=(n_p // nb_p) // tm, nb_p=nb_p, nb_s=nb_s, s_len=n_s // nb_s)
    return pl.pallas_call(
        kern,
        grid=(n_full + 1, n // tn, nk),
        in_specs=[pl.BlockSpec((tm, tk), lambda i, j, kk: (i, kk)),
                  pl.BlockSpec((tk, tn), lambda i, j, kk: (kk, j)),
                  pl.BlockSpec((tm, tn), lambda i, j, kk: (jnp.minimum(i, n_full - 1), j)),
                  pl.BlockSpec((n_s, tn), lambda i, j, kk: (n_p // n_s, j)),
                  pl.BlockSpec((MOD_ROWS, tn), lambda i, j, kk: (0, gate_off // tn + j))],
        out_specs=pl.BlockSpec((tm, tn), lambda i, j, kk: (i, j)),
        out_shape=jax.ShapeDtypeStruct((n_p + n_s, n), F32),
        scratch_shapes=[pltpu.VMEM((tm, tn), F32)],
        compiler_params=_cparams(3),
        name="ffn_down",
    )(a, w, x_all, x_all, mod)


def _final_norm_kernel(x_ref, g_ref, o_ref):
    o_ref[...] = _rms(x_ref[...], g_ref[...])


def _final_norm(x_all, g, *, tr, row_block0, n_tiles):
    d = x_all.shape[1]
    return pl.pallas_call(
        _final_norm_kernel,
        grid=(n_tiles,),
        in_specs=[pl.BlockSpec((tr, d), lambda i: (row_block0 + i, 0)),
                  pl.BlockSpec((1, d), lambda i: (0, 0))],
        out_specs=pl.BlockSpec((tr, d), lambda i: (i, 0)),
        out_shape=jax.ShapeDtypeStruct((n_tiles * tr, d), F32),
        compiler_params=_cparams(1),
        name="final_norm",
    )(x_all, g.reshape(1, d))


def _rope_tables(pos):
    half = MLA_ROPE // 2
    inv = 1.0 / (ROPE_THETA ** (jnp.arange(half, dtype=F32) / half))
    ang = pos.astype(F32)[:, None] * inv[None, :]
    cos, sin = jnp.cos(ang), jnp.sin(ang)
    return jnp.concatenate([cos, cos], axis=-1), jnp.concatenate([sin, sin], axis=-1)


def _largest_tile(n, cap, unit=LANE):
    best = unit
    for m in range(unit, cap + 1, unit):
        if n % m == 0:
            best = m
    return best


def kernel(x_prompt, x_sample, cache_mla_latent, cache_mla_krope, cache_sb_k, cache_sb_v, c_prompt, c_sample, w_ada, b_ada, g_mix, g_ffn, w_in, g_q_lat, g_kv_lat, w_uq, w_uk, w_uv, g_out_mla, g_out_sb, w_out, w_gate, w_up, w_down, g_final):
    depth = w_in.shape[0]
    assert depth == 1, "single-layer stack"
    nb_p, s_p, d = x_prompt.shape
    nb_s, s_s, _ = x_sample.shape
    past = cache_mla_latent.shape[2]
    d_ff = w_gate.shape[2]
    n_p, n_s = nb_p * s_p, nb_s * s_s
    assert nb_p + nb_s <= MOD_ROWS and n_p % n_s == 0

    tm = min(1024, s_p)
    tr = min(512, s_p)
    tq_mla = min(512, s_p)
    tq_sb = min(256, s_p)
    assert s_p % tm == 0 and s_p % tr == 0 and n_p % tm == 0

    xp = x_prompt.reshape(n_p, d)
    xs = x_sample.reshape(n_s, d)

    c_all = jnp.concatenate(
        [c_prompt, c_sample, jnp.zeros((MOD_ROWS - nb_p - nb_s, d), F32)], axis=0)
    mod = _adaln(c_all, w_ada[0], b_ada[0])
    sh_m, sc_m, gt_m, sh_f, sc_f, gt_f = (k * d for k in range(N_MOD))

    pos_all = jnp.concatenate([jnp.tile(jnp.arange(s_p), nb_p),
                               jnp.tile(past + jnp.arange(s_s), nb_s)])
    cos2, sin2 = _rope_tables(pos_all)

    h = _norm_mod(xp, xs, 0, mod, g_mix[0], n_p=n_p, n_s=n_s, nb_p=nb_p, nb_s=nb_s,
                  sh_off=sh_m, sc_off=sc_m, tr=tr)
    w_in0 = w_in[0]
    na = -(-MLA_IN // LANE) * LANE
    w_a = w_in0[:, :na].astype(BF16)
    w_sb = w_in0[:, MLA_IN:].astype(BF16)
    qn, lat, kr = _inproj_a(h, w_a, g_q_lat[0], g_kv_lat[0], cos2, sin2, n_p=n_p, n_s=n_s, tm=tr)

    tn_sb = 512
    nsb = SB_WIDTH // tn_sb
    nft = n_p // tm
    sbq = _mm(h, w_sb, tm=tm, row_block0=0, n_full=nft, last_rows=n_s, tn=tn_sb, col_block0=0,
              n_cols=SB_WIDTH, out_dtype=BF16, scale=SB_SCALE, name="sb_q")
    p_sbk = _mm(h, w_sb, tm=tm, row_block0=0, n_full=nft, last_rows=0, tn=tn_sb,
                col_block0=nsb, n_cols=SB_WIDTH, out_dtype=F32, name="sb_k_prompt")
    p_sbv = _mm(h, w_sb, tm=tm, row_block0=0, n_full=nft, last_rows=0, tn=tn_sb,
                col_block0=2 * nsb, n_cols=SB_WIDTH, out_dtype=F32, name="sb_v_prompt")
    s_sbk = _mm(h, w_sb, tm=n_s, row_block0=n_p // n_s, n_full=1, last_rows=0, tn=tn_sb,
                col_block0=nsb, n_cols=SB_WIDTH, out_dtype=F32, name="sb_k_sample")
    s_sbv = _mm(h, w_sb, tm=n_s, row_block0=n_p // n_s, n_full=1, last_rows=0, tn=tn_sb,
                col_block0=2 * nsb, n_cols=SB_WIDTH, out_dtype=F32, name="sb_v_sample")

    w_uq_t = jnp.transpose(w_uq[0], (1, 0, 2)).astype(BF16)
    q = _qheads(qn, w_uq_t, cos2, sin2, n_p=n_p, n_s=n_s, tm=tm)
    wk = w_uk[0].reshape(KV_LORA, MLA_HEADS * MLA_NOPE).astype(BF16)
    wv = w_uv[0].reshape(KV_LORA, MLA_WIDTH).astype(BF16)
    kcat, v_p = _kv_up(lat, kr, wk, wv, n_p=n_p, tm=tr)
    op_mla = _mla_prompt(q, kcat, v_p, n_p, nb=nb_p, s_len=s_p, tq=tq_mla)

    w_uk_t = jnp.transpose(w_uk[0], (1, 2, 0)).astype(BF16)
    qabs = _qabs(q, w_uk_t, n_p=n_p, n_s=n_s)
    os_mla = _mla_sample(qabs, q, cache_mla_latent[0], cache_mla_krope[0], lat, kr, wv,
                         n_p=n_p, nb=nb_s, s_len=s_s)

    op_sb = _sb_prompt(sbq, p_sbk, p_sbv, n_p, nb=nb_p, s_len=s_p, tq=tq_sb)
    ck = cache_sb_k[0].reshape(nb_s, past, SB_WIDTH)
    cv = cache_sb_v[0].reshape(nb_s, past, SB_WIDTH)
    os_sb = _sb_sample(sbq, s_sbk, s_sbv, ck, cv, n_p=n_p, nb=nb_s, s_len=s_s,
                       tk=min(512, past))

    merged = _merge(op_mla, op_sb, os_mla, os_sb, g_out_mla[0], g_out_sb[0],
                    n_p=n_p, n_s=n_s, tr=tr)
    x1 = _mm_res(merged, w_out[0], xp, xs, 0, mod, gt_m, n_p=n_p, n_s=n_s, nb_p=nb_p,
                 nb_s=nb_s, tm=tm, tn=min(512, d))

    h2 = _norm_mod(x1, x1, n_p // n_s, mod, g_ffn[0], n_p=n_p, n_s=n_s, nb_p=nb_p, nb_s=nb_s,
                   sh_off=sh_f, sc_off=sc_f, tr=tr)
    act = _swiglu(h2, w_gate[0], w_up[0], n_p=n_p, n_s=n_s, tm=tm, tn=_largest_tile(d_ff, 256))
    tk_down = d_ff // 2 if (d_ff // 2) % LANE == 0 else d_ff
    x2 = _down(act, w_down[0], x1, mod, gt_f, n_p=n_p, n_s=n_s, nb_p=nb_p, nb_s=nb_s,
               tm=tm, tn=min(256, d), tk=tk_down)

    y_p = _final_norm(x2, g_final, tr=tr, row_block0=0, n_tiles=n_p // tr)
    y_s = _final_norm(x2, g_final, tr=n_s, row_block0=n_p // n_s, n_tiles=1)

    return (y_p.reshape(nb_p, s_p, d), y_s.reshape(nb_s, s_s, d),
            lat[:n_p].reshape(1, nb_p, s_p, KV_LORA),
            kr[:n_p].reshape(1, nb_p, s_p, MLA_ROPE),
            p_sbk.reshape(1, nb_p, s_p, SB_HEADS, SB_DIM),
            p_sbv.reshape(1, nb_p, s_p, SB_HEADS, SB_DIM),
            lat[n_p:].reshape(1, nb_s, s_s, KV_LORA),
            kr[n_p:].reshape(1, nb_s, s_s, MLA_ROPE),
            s_sbk.reshape(1, nb_s, s_s, SB_HEADS, SB_DIM),
            s_sbv.reshape(1, nb_s, s_s, SB_HEADS, SB_DIM))
```

```python
import functools

import jax
import jax.numpy as jnp
from jax import lax
from jax.experimental import pallas as pl
from jax.experimental.pallas import tpu as pltpu

F32 = jnp.float32
BF16 = jnp.bfloat16

CHUNK = 64
EPS = 1e-6
ROPE_THETA = 10000.0
MLA_HEADS = 16
MLA_NOPE = 128
MLA_ROPE = 64
MLA_QK = MLA_NOPE + MLA_ROPE
MLA_V = 128
Q_LORA = 1024
KV_LORA = 512
MLA_IN = Q_LORA + KV_LORA + MLA_ROPE
MLA_WIDTH = MLA_HEADS * MLA_V
MLA_SCALE = float(MLA_QK) ** -0.5
LOG2E = 1.4426950408889634
MLA_Q_SCALE = MLA_SCALE * LOG2E
SB_HEADS = 16
SB_DIM = 128
SB_WIDTH = SB_HEADS * SB_DIM
SB_SCALE = float(SB_DIM) ** -0.5
SB_Q_SCALE = SB_SCALE * LOG2E
N_MOD = 6
MOD_ROWS = 16

LANE = 128
VMEM_LIMIT = 56 * 1024 * 1024


def _cparams(n_axes, vmem=VMEM_LIMIT):
    return pltpu.CompilerParams(
        dimension_semantics=("arbitrary",) * n_axes, vmem_limit_bytes=vmem)


def _dot(a, b):
    return jnp.dot(a, b, preferred_element_type=F32)


def _dot_nt(a, b):
    return lax.dot_general(a, b, (((1,), (1,)), ((), ())), preferred_element_type=F32)


def _rms(x, g):
    ms = jnp.mean(x * x, axis=-1, keepdims=True)
    return x * lax.rsqrt(ms + EPS) * g


def _rope(t, cos2, sin2):
    half = MLA_ROPE // 2
    rot = jnp.concatenate([-t[:, half:], t[:, :half]], axis=-1)
    return t * cos2 + rot * sin2


def _row_tiles(i, n_full, tm, last_rows, body):
    @pl.when(i < n_full)
    def _():
        body(slice(0, tm), False)

    if last_rows:
        @pl.when(i == n_full)
        def _():
            body(slice(0, last_rows), True)


def _adaln_kernel(c_ref, w_ref, b_ref, o_ref):
    c = c_ref[...]
    s = (c * jax.nn.sigmoid(c)).astype(BF16)
    o_ref[...] = _dot(s, w_ref[...].astype(BF16)) + b_ref[...]


def _adaln(c_all, w_ada, b_ada):
    d, n = w_ada.shape
    tn = _largest_tile(n, 1024)
    return pl.pallas_call(
        _adaln_kernel,
        grid=(n // tn,),
        in_specs=[pl.BlockSpec((MOD_ROWS, d), lambda j: (0, 0)),
                  pl.BlockSpec((d, tn), lambda j: (0, j)),
                  pl.BlockSpec((1, tn), lambda j: (0, j))],
        out_specs=pl.BlockSpec((MOD_ROWS, tn), lambda j: (0, j)),
        out_shape=jax.ShapeDtypeStruct((MOD_ROWS, n), F32),
        compiler_params=_cparams(1),
        name="adaln",
    )(c_all, w_ada, b_ada.reshape(1, n))


def _norm_mod_kernel(xp_ref, xs_ref, mod_ref, g_ref, o_ref, *, n_full, tr, tiles_per_b,
                     nb_p, nb_s, s_len, sh_off, sc_off, d):
    i = pl.program_id(0)
    g = g_ref[...]

    @pl.when(i < n_full)
    def _():
        b = i // tiles_per_b
        sh = mod_ref[pl.ds(b, 1), sh_off:sh_off + d]
        sc = mod_ref[pl.ds(b, 1), sc_off:sc_off + d]
        o_ref[...] = (_rms(xp_ref[...], g) * (1.0 + sc) + sh).astype(o_ref.dtype)

    @pl.when(i == n_full)
    def _():
        for bs in range(nb_s):
            r = nb_p + bs
            sh = mod_ref[r:r + 1, sh_off:sh_off + d]
            sc = mod_ref[r:r + 1, sc_off:sc_off + d]
            rows = slice(bs * s_len, (bs + 1) * s_len)
            o_ref[rows, :] = (_rms(xs_ref[rows, :], g) * (1.0 + sc) + sh).astype(o_ref.dtype)


def _norm_mod(xp, xs, xs_block, mod, g, *, n_p, n_s, nb_p, nb_s, sh_off, sc_off, tr):
    d = xp.shape[1]
    n_full = n_p // tr
    kern = functools.partial(
        _norm_mod_kernel, n_full=n_full, tr=tr, tiles_per_b=(n_p // nb_p) // tr,
        nb_p=nb_p, nb_s=nb_s, s_len=n_s // nb_s, sh_off=sh_off, sc_off=sc_off, d=d)
    return pl.pallas_call(
        kern,
        grid=(n_full + 1,),
        in_specs=[pl.BlockSpec((tr, d), lambda i: (jnp.minimum(i, n_full - 1), 0)),
                  pl.BlockSpec((n_s, d), lambda i: (xs_block, 0)),
                  pl.BlockSpec(mod.shape, lambda i: (0, 0)),
                  pl.BlockSpec((1, d), lambda i: (0, 0))],
        out_specs=pl.BlockSpec((tr, d), lambda i: (i, 0)),
        out_shape=jax.ShapeDtypeStruct((n_p + n_s, d), BF16),
        compiler_params=_cparams(1),
        name="norm_mod",
    )(xp, xs, mod, g.reshape(1, d))


def _inproj_a_kernel(a_ref, w_ref, gq_ref, gkv_ref, cos_ref, sin_ref,
                     qn_ref, lat_ref, kr_ref, *, n_full, tm, last_rows):
    i = pl.program_id(0)

    def body(rows, _):
        r = _dot(a_ref[rows, :], w_ref[...])
        qn_ref[rows, :] = _rms(r[:, :Q_LORA], gq_ref[...]).astype(qn_ref.dtype)
        lat_ref[rows, :] = _rms(r[:, Q_LORA:Q_LORA + KV_LORA], gkv_ref[...])
        kr_ref[rows, :] = _rope(r[:, Q_LORA + KV_LORA:MLA_IN], cos_ref[rows, :], sin_ref[rows, :])

    _row_tiles(i, n_full, tm, last_rows, body)


def _inproj_a(h, w_a, g_q, g_kv, cos2, sin2, *, n_p, n_s, tm):
    t, d = h.shape
    n_full = n_p // tm
    na = w_a.shape[1]
    kern = functools.partial(_inproj_a_kernel, n_full=n_full, tm=tm, last_rows=n_s)
    row = lambda i: (i, 0)
    fixed = lambda i: (0, 0)
    return pl.pallas_call(
        kern,
        grid=(n_full + 1,),
        in_specs=[pl.BlockSpec((tm, d), row),
                  pl.BlockSpec((d, na), fixed),
                  pl.BlockSpec((1, Q_LORA), fixed),
                  pl.BlockSpec((1, KV_LORA), fixed),
                  pl.BlockSpec((tm, MLA_ROPE), row),
                  pl.BlockSpec((tm, MLA_ROPE), row)],
        out_specs=[pl.BlockSpec((tm, Q_LORA), row),
                   pl.BlockSpec((tm, KV_LORA), row),
                   pl.BlockSpec((tm, MLA_ROPE), row)],
        out_shape=[jax.ShapeDtypeStruct((t, Q_LORA), BF16),
                   jax.ShapeDtypeStruct((t, KV_LORA), F32),
                   jax.ShapeDtypeStruct((t, MLA_ROPE), F32)],
        compiler_params=_cparams(1),
        name="inproj_a",
    )(h, w_a, g_q.reshape(1, -1), g_kv.reshape(1, -1), cos2, sin2)


def _mm_kernel(a_ref, w_ref, o_ref, *, n_full, tm, last_rows, scale):
    i = pl.program_id(0)
    w = w_ref[...].astype(BF16)

    def body(rows, _):
        r = _dot(a_ref[rows, :], w)
        if scale is not None:
            r = r * scale
        o_ref[rows, :] = r.astype(o_ref.dtype)

    _row_tiles(i, n_full, tm, last_rows, body)


def _mm(a, w, *, tm, row_block0, n_full, last_rows, tn, col_block0, n_cols, out_dtype,
        scale=None, name="mm"):
    k = a.shape[1]
    n_row_tiles = n_full + (1 if last_rows else 0)
    out_rows = n_full * tm + last_rows
    kern = functools.partial(_mm_kernel, n_full=n_full, tm=tm, last_rows=last_rows, scale=scale)
    return pl.pallas_call(
        kern,
        grid=(n_row_tiles, n_cols // tn),
        in_specs=[pl.BlockSpec((tm, k), lambda i, j: (row_block0 + i, 0)),
                  pl.BlockSpec((k, tn), lambda i, j: (0, col_block0 + j))],
        out_specs=pl.BlockSpec((tm, tn), lambda i, j: (i, j)),
        out_shape=jax.ShapeDtypeStruct((out_rows, n_cols), out_dtype),
        compiler_params=_cparams(2),
        name=name,
    )(a, w)


def _qheads_kernel(a_ref, w_ref, cost_ref, sint_ref, cos_ref, sin_ref, qt_ref, qs_ref,
                   *, n_full, last_rows):
    i = pl.program_id(1)
    half = MLA_ROPE // 2

    @pl.when(i < n_full)
    def _():
        qt = _dot_nt(w_ref[0], a_ref[...])
        qt_ref[0, :MLA_NOPE, :] = (qt[:MLA_NOPE] * MLA_Q_SCALE).astype(qt_ref.dtype)
        x1 = qt[MLA_NOPE:MLA_NOPE + half]
        x2 = qt[MLA_NOPE + half:]
        c, s = cost_ref[...], sint_ref[...]
        qt_ref[0, MLA_NOPE:MLA_NOPE + half, :] = ((x1 * c - x2 * s) * MLA_Q_SCALE).astype(qt_ref.dtype)
        qt_ref[0, MLA_NOPE + half:, :] = ((x2 * c + x1 * s) * MLA_Q_SCALE).astype(qt_ref.dtype)

    @pl.when(i == n_full)
    def _():
        rows = slice(0, last_rows)
        q = _dot_nt(a_ref[rows, :], w_ref[0])
        qs_ref[0, :, :MLA_NOPE] = (q[:, :MLA_NOPE] * MLA_Q_SCALE).astype(qs_ref.dtype)
        qr = _rope(q[:, MLA_NOPE:], cos_ref[...], sin_ref[...])
        qs_ref[0, :, MLA_NOPE:] = (qr * MLA_Q_SCALE).astype(qs_ref.dtype)


def _qheads(qn, w_uq_ht, cos_t, sin_t, cos2, sin2, *, n_p, n_s, tm):
    t, k = qn.shape
    n_full = n_p // tm
    half = MLA_ROPE // 2
    kern = functools.partial(_qheads_kernel, n_full=n_full, last_rows=n_s)
    pcol = lambda h, i: (0, jnp.minimum(i, n_full - 1))
    return pl.pallas_call(
        kern,
        grid=(MLA_HEADS, n_full + 1),
        in_specs=[pl.BlockSpec((tm, k), lambda h, i: (i, 0)),
                  pl.BlockSpec((1, MLA_QK, k), lambda h, i: (h, 0, 0)),
                  pl.BlockSpec((half, tm), pcol),
                  pl.BlockSpec((half, tm), pcol),
                  pl.BlockSpec((n_s, MLA_ROPE), lambda h, i: (n_p // n_s, 0)),
                  pl.BlockSpec((n_s, MLA_ROPE), lambda h, i: (n_p // n_s, 0))],
        out_specs=[pl.BlockSpec((1, MLA_QK, tm), lambda h, i: (h, 0, jnp.minimum(i, n_full - 1))),
                   pl.BlockSpec((1, n_s, MLA_QK), lambda h, i: (h, 0, 0))],
        out_shape=[jax.ShapeDtypeStruct((MLA_HEADS, MLA_QK, n_p), BF16),
                   jax.ShapeDtypeStruct((MLA_HEADS, n_s, MLA_QK), BF16)],
        compiler_params=_cparams(2),
        name="qheads",
    )(qn, w_uq_ht, cos_t, sin_t, cos2, sin2)


def _kv_up_kernel(lat_ref, kr_ref, wk_ref, wvt_ref, k_ref, vt_ref):
    lat = lat_ref[...].astype(BF16)
    kn = _dot(lat, wk_ref[...]).astype(BF16)
    vt = _dot_nt(wvt_ref[...], lat).astype(BF16)
    kr = kr_ref[...].astype(BF16)
    for h in range(MLA_HEADS):
        k_ref[h, :, :MLA_NOPE] = kn[:, h * MLA_NOPE:(h + 1) * MLA_NOPE]
        k_ref[h, :, MLA_NOPE:] = kr
        vt_ref[h, 0] = vt[h * MLA_V:(h + 1) * MLA_V, :]


def _kv_up(lat, kr, wk, wvt, *, n_p, tk):
    fixed = lambda i: (0, 0)
    return pl.pallas_call(
        _kv_up_kernel,
        grid=(n_p // tk,),
        in_specs=[pl.BlockSpec((tk, KV_LORA), lambda i: (i, 0)),
                  pl.BlockSpec((tk, MLA_ROPE), lambda i: (i, 0)),
                  pl.BlockSpec(wk.shape, fixed),
                  pl.BlockSpec(wvt.shape, fixed)],
        out_specs=[pl.BlockSpec((MLA_HEADS, tk, MLA_QK), lambda i: (0, i, 0)),
                   pl.BlockSpec((MLA_HEADS, 1, MLA_V, tk), lambda i: (0, i, 0, 0))],
        out_shape=[jax.ShapeDtypeStruct((MLA_HEADS, n_p, MLA_QK), BF16),
                   jax.ShapeDtypeStruct((MLA_HEADS, n_p // tk, MLA_V, tk), BF16)],
        compiler_params=_cparams(1),
        name="kv_up",
    )(lat, kr, wk, wvt)


def _mla_prompt_kernel(qt_ref, k_ref, vt_ref, o_ref, m_sc, l_sc, acc_sc, *, tq, hp):
    qi = pl.program_id(2)
    for hh in range(hp):
        m_sc[hh] = jnp.full((1, tq), -jnp.inf, F32)
        l_sc[hh] = jnp.zeros((1, tq), F32)
        acc_sc[hh] = jnp.zeros((MLA_V, tq), F32)

    def step(j, diag):
        start = pl.multiple_of(j * tq, tq)
        for hh in range(hp):
            s = _dot(k_ref[hh, pl.ds(start, tq), :], qt_ref[hh])
            if diag:
                kc = lax.broadcasted_iota(jnp.int32, s.shape, 0) // CHUNK
                qc = lax.broadcasted_iota(jnp.int32, s.shape, 1) // CHUNK
                s = jnp.where(kc <= qc, s, -jnp.inf)
            m_prev = m_sc[hh]
            m_new = jnp.maximum(m_prev, jnp.max(s, axis=0, keepdims=True))
            alpha = jnp.exp2(m_prev - m_new)
            p = jnp.exp2(s - m_new)
            l_sc[hh] = alpha * l_sc[hh] + jnp.sum(p, axis=0, keepdims=True)
            acc_sc[hh] = alpha * acc_sc[hh] + _dot(vt_ref[hh, j], p.astype(BF16))
            m_sc[hh] = m_new

    def loop_body(j, c):
        step(j, False)
        return c

    lax.fori_loop(0, qi, loop_body, 0)
    step(qi, True)
    for hh in range(hp):
        o_ref[:, hh * MLA_V:(hh + 1) * MLA_V] = (acc_sc[hh] / l_sc[hh]).T


def _mla_prompt(qt, kcat, vt, *, nb, s_len, tq, hp):
    nq = s_len // tq
    kern = functools.partial(_mla_prompt_kernel, tq=tq, hp=hp)
    return pl.pallas_call(
        kern,
        grid=(nb, MLA_HEADS // hp, nq),
        in_specs=[pl.BlockSpec((hp, MLA_QK, tq), lambda b, g, i: (g, 0, b * nq + i)),
                  pl.BlockSpec((hp, s_len, MLA_QK), lambda b, g, i: (g, b, 0)),
                  pl.BlockSpec((hp, nq, MLA_V, tq), lambda b, g, i: (g, b, 0, 0))],
        out_specs=pl.BlockSpec((tq, hp * MLA_V), lambda b, g, i: (b * nq + i, g)),
        out_shape=jax.ShapeDtypeStruct((nb * s_len, MLA_WIDTH), F32),
        scratch_shapes=[pltpu.VMEM((hp, 1, tq), F32), pltpu.VMEM((hp, 1, tq), F32),
                        pltpu.VMEM((hp, MLA_V, tq), F32)],
        compiler_params=_cparams(3),
        name="mla_prompt",
    )(qt, kcat, vt)


def _sb_tile(q, k, v, carry, mask, ntri):
    z = _dot_nt(q, k)
    sp = jnp.maximum(z, 0.0) + jnp.log2(1.0 + jnp.exp2(-jnp.abs(z)))
    if mask is not None:
        sp = jnp.where(mask, sp, 0.0)
    hi = sp.astype(BF16)
    lo = (sp - hi.astype(F32)).astype(BF16)
    csum = _dot(hi, ntri) + _dot(lo, ntri)
    a = jnp.exp2(z + csum + carry)
    if mask is not None:
        a = jnp.where(mask, a, 0.0)
    return _dot(a.astype(BF16), v), carry + csum[:, 0:1]


def _ntri(n):
    r = lax.broadcasted_iota(jnp.int32, (n, n), 0)
    c = lax.broadcasted_iota(jnp.int32, (n, n), 1)
    return jnp.where(r >= c, -1.0, 0.0).astype(BF16)


def _sb_prompt_kernel(q_ref, k_ref, v_ref, o_ref, carry_sc, acc_sc, *, tq, tk):
    qi = pl.program_id(2)
    q = q_ref[...]
    ntri = _ntri(tk)
    nd = tq // tk
    row = lax.broadcasted_iota(jnp.int32, (tq, tk), 0)
    col = lax.broadcasted_iota(jnp.int32, (tq, tk), 1)

    def tile(j):
        start = pl.multiple_of(j * tk, tk)
        return (k_ref[pl.ds(start, tk), :].astype(BF16), v_ref[pl.ds(start, tk), :].astype(BF16))

    carry_sc[...] = jnp.zeros(carry_sc.shape, F32)
    acc_sc[...] = jnp.zeros(acc_sc.shape, F32)
    for d in range(nd - 1, -1, -1):
        k, v = tile(qi * nd + d)
        o, c = _sb_tile(q, k, v, carry_sc[...], (col + d * tk) < row, ntri)
        acc_sc[...] += o
        carry_sc[...] = c

    def loop_body(jj, c0):
        k, v = tile(qi * nd - 1 - jj)
        o, c = _sb_tile(q, k, v, carry_sc[...], None, ntri)
        acc_sc[...] += o
        carry_sc[...] = c
        return c0

    lax.fori_loop(0, qi * nd, loop_body, 0)
    o_ref[...] = acc_sc[...]


def _sb_prompt(q, k, v, *, nb, s_len, tq, tk):
    nq = s_len // tq
    kern = functools.partial(_sb_prompt_kernel, tq=tq, tk=tk)
    return pl.pallas_call(
        kern,
        grid=(nb, SB_HEADS, nq),
        in_specs=[pl.BlockSpec((tq, SB_DIM), lambda b, h, i: (b * nq + i, h)),
                  pl.BlockSpec((s_len, SB_DIM), lambda b, h, i: (b, h)),
                  pl.BlockSpec((s_len, SB_DIM), lambda b, h, i: (b, h))],
        out_specs=pl.BlockSpec((tq, SB_DIM), lambda b, h, i: (b * nq + i, h)),
        out_shape=jax.ShapeDtypeStruct((nb * s_len, SB_WIDTH), F32),
        scratch_shapes=[pltpu.VMEM((tq, 1), F32), pltpu.VMEM((tq, SB_DIM), F32)],
        compiler_params=_cparams(3),
        name="sb_prompt",
    )(q, k, v)


def _qabs_kernel(q_ref, w_ref, o_ref):
    o_ref[0] = _dot(q_ref[0][:, :MLA_NOPE], w_ref[0]).astype(o_ref.dtype)


def _qabs(q, w_uk_t, *, n_s):
    return pl.pallas_call(
        _qabs_kernel,
        grid=(MLA_HEADS,),
        in_specs=[pl.BlockSpec((1, n_s, MLA_QK), lambda h: (h, 0, 0)),
                  pl.BlockSpec((1, MLA_NOPE, KV_LORA), lambda h: (h, 0, 0))],
        out_specs=pl.BlockSpec((1, n_s, KV_LORA), lambda h: (h, 0, 0)),
        out_shape=jax.ShapeDtypeStruct((MLA_HEADS, n_s, KV_LORA), BF16),
        compiler_params=_cparams(1),
        name="qabs",
    )(q, w_uk_t)


def _mla_sample_kernel(qa_ref, q_ref, clat_ref, ckr_ref, nlat_ref, nkr_ref, wv_ref, o_ref,
                       *, past, s_len):
    hm = MLA_HEADS * s_len
    qa = qa_ref[...].reshape(hm, KV_LORA)
    qr = q_ref[...][:, :, MLA_NOPE:].reshape(hm, MLA_ROPE)
    clat = clat_ref[0].astype(BF16)
    ckr = ckr_ref[0].astype(BF16)
    nlat = nlat_ref[...].astype(BF16)
    nkr = nkr_ref[...].astype(BF16)
    s_c = _dot_nt(qa, clat) + _dot_nt(qr, ckr)
    s_n = _dot_nt(qa, nlat) + _dot_nt(qr, nkr)
    t = lax.broadcasted_iota(jnp.int32, s_n.shape, 0) % s_len
    s = lax.broadcasted_iota(jnp.int32, s_n.shape, 1)
    s_n = jnp.where((past + s) // CHUNK <= (past + t) // CHUNK, s_n, -jnp.inf)
    m = jnp.maximum(jnp.max(s_c, axis=-1, keepdims=True), jnp.max(s_n, axis=-1, keepdims=True))
    p_c = jnp.exp2(s_c - m)
    p_n = jnp.exp2(s_n - m)
    l = jnp.sum(p_c, axis=-1, keepdims=True) + jnp.sum(p_n, axis=-1, keepdims=True)
    o_lat = (_dot(p_c.astype(BF16), clat) + _dot(p_n.astype(BF16), nlat)) / l
    o_lat = o_lat.astype(BF16)
    for h in range(MLA_HEADS):
        o_ref[:, h * MLA_V:(h + 1) * MLA_V] = _dot(
            o_lat[h * s_len:(h + 1) * s_len, :], wv_ref[:, h * MLA_V:(h + 1) * MLA_V])


def _mla_sample(qabs, q, c_lat, c_kr, lat, kr, wv, *, n_p, nb, s_len):
    past = c_lat.shape[1]
    rb0 = n_p // s_len
    kern = functools.partial(_mla_sample_kernel, past=past, s_len=s_len)
    return pl.pallas_call(
        kern,
        grid=(nb,),
        in_specs=[pl.BlockSpec((MLA_HEADS, s_len, KV_LORA), lambda b: (0, b, 0)),
                  pl.BlockSpec((MLA_HEADS, s_len, MLA_QK), lambda b: (0, b, 0)),
                  pl.BlockSpec((1, past, KV_LORA), lambda b: (b, 0, 0)),
                  pl.BlockSpec((1, past, MLA_ROPE), lambda b: (b, 0, 0)),
                  pl.BlockSpec((s_len, KV_LORA), lambda b: (rb0 + b, 0)),
                  pl.BlockSpec((s_len, MLA_ROPE), lambda b: (rb0 + b, 0)),
                  pl.BlockSpec(wv.shape, lambda b: (0, 0))],
        out_specs=pl.BlockSpec((s_len, MLA_WIDTH), lambda b: (b, 0)),
        out_shape=jax.ShapeDtypeStruct((nb * s_len, MLA_WIDTH), F32),
        compiler_params=_cparams(1),
        name="mla_sample",
    )(qabs, q, c_lat, c_kr, lat, kr, wv)


def _sb_sample_kernel(q_ref, nk_ref, nv_ref, ck_ref, cv_ref, o_ref, carry_sc, acc_sc,
                      *, s_len, tk):
    j = pl.program_id(1)
    nj = pl.num_programs(1)
    ntri = _ntri(tk)

    @pl.when(j == 0)
    def _():
        ntri_n = _ntri(s_len)
        row = lax.broadcasted_iota(jnp.int32, (s_len, s_len), 0)
        col = lax.broadcasted_iota(jnp.int32, (s_len, s_len), 1)
        for h in range(SB_HEADS):
            cols = slice(h * SB_DIM, (h + 1) * SB_DIM)
            o, c = _sb_tile(q_ref[:, cols], nk_ref[:, cols].astype(BF16),
                            nv_ref[:, cols].astype(BF16),
                            jnp.zeros((s_len, 1), F32), col < row, ntri_n)
            acc_sc[:, cols] = o
            carry_sc[:, h:h + 1] = c

    for h in range(SB_HEADS):
        cols = slice(h * SB_DIM, (h + 1) * SB_DIM)
        o, c = _sb_tile(q_ref[:, cols], ck_ref[0, :, h, :].astype(BF16),
                        cv_ref[0, :, h, :].astype(BF16), carry_sc[:, h:h + 1], None, ntri)
        acc_sc[:, cols] += o
        carry_sc[:, h:h + 1] = c

    @pl.when(j == nj - 1)
    def _():
        o_ref[...] = acc_sc[...]


def _sb_sample(q, nk, nv, ck, cv, *, n_p, nb, s_len, tk):
    past = ck.shape[1]
    nkt = past // tk
    rb0 = n_p // s_len
    kern = functools.partial(_sb_sample_kernel, s_len=s_len, tk=tk)
    cache = pl.BlockSpec((1, tk, SB_HEADS, SB_DIM), lambda b, j: (b, nkt - 1 - j, 0, 0))
    return pl.pallas_call(
        kern,
        grid=(nb, nkt),
        in_specs=[pl.BlockSpec((s_len, SB_WIDTH), lambda b, j: (rb0 + b, 0)),
                  pl.BlockSpec((s_len, SB_WIDTH), lambda b, j: (b, 0)),
                  pl.BlockSpec((s_len, SB_WIDTH), lambda b, j: (b, 0)),
                  cache, cache],
        out_specs=pl.BlockSpec((s_len, SB_WIDTH), lambda b, j: (b, 0)),
        out_shape=jax.ShapeDtypeStruct((nb * s_len, SB_WIDTH), F32),
        scratch_shapes=[pltpu.VMEM((s_len, LANE), F32), pltpu.VMEM((s_len, SB_WIDTH), F32)],
        compiler_params=_cparams(2),
        name="sb_sample",
    )(q, nk, nv, ck, cv)


def _merge_kernel(op_mla_ref, op_sb_ref, os_mla_ref, os_sb_ref, ga_ref, gb_ref, o_ref,
                  *, n_full, n_s):
    i = pl.program_id(0)

    def put(rows, a, b):
        o_ref[rows, :MLA_WIDTH] = _rms(a, ga_ref[...]).astype(o_ref.dtype)
        o_ref[rows, MLA_WIDTH:] = _rms(b, gb_ref[...]).astype(o_ref.dtype)

    @pl.when(i < n_full)
    def _():
        put(slice(None), op_mla_ref[...], op_sb_ref[...])

    @pl.when(i == n_full)
    def _():
        put(slice(0, n_s), os_mla_ref[...], os_sb_ref[...])


def _merge(op_mla, op_sb, os_mla, os_sb, g_a, g_b, *, n_p, n_s, tr):
    n_full = n_p // tr
    kern = functools.partial(_merge_kernel, n_full=n_full, n_s=n_s)
    prow = lambda i: (jnp.minimum(i, n_full - 1), 0)
    fixed = lambda i: (0, 0)
    return pl.pallas_call(
        kern,
        grid=(n_full + 1,),
        in_specs=[pl.BlockSpec((tr, MLA_WIDTH), prow),
                  pl.BlockSpec((tr, SB_WIDTH), prow),
                  pl.BlockSpec((n_s, MLA_WIDTH), fixed),
                  pl.BlockSpec((n_s, SB_WIDTH), fixed),
                  pl.BlockSpec((1, MLA_WIDTH), fixed),
                  pl.BlockSpec((1, SB_WIDTH), fixed)],
        out_specs=pl.BlockSpec((tr, MLA_WIDTH + SB_WIDTH), lambda i: (i, 0)),
        out_shape=jax.ShapeDtypeStruct((n_p + n_s, MLA_WIDTH + SB_WIDTH), BF16),
        compiler_params=_cparams(1),
        name="merge_norm",
    )(op_mla, op_sb, os_mla, os_sb, g_a.reshape(1, -1), g_b.reshape(1, -1))


def _gated_rows(o_ref, xp_ref, xs_ref, mod_ref, r, rows, is_sample, i, *, tiles_per_b,
                nb_p, nb_s, s_len):
    if not is_sample:
        b = i // tiles_per_b
        o_ref[...] = xp_ref[...] + mod_ref[pl.ds(b, 1), :] * r
    else:
        for bs in range(nb_s):
            rr = slice(bs * s_len, (bs + 1) * s_len)
            g = mod_ref[nb_p + bs:nb_p + bs + 1, :]
            o_ref[rr, :] = xs_ref[rr, :] + g * r[rr, :]


def _mm_res_kernel(a_ref, w_ref, xp_ref, xs_ref, mod_ref, o_ref, *, n_full, tm, last_rows, **kw):
    i = pl.program_id(0)
    w = w_ref[...].astype(BF16)

    def body(rows, is_sample):
        r = _dot(a_ref[rows, :], w)
        _gated_rows(o_ref, xp_ref, xs_ref, mod_ref, r, rows, is_sample, i, **kw)

    _row_tiles(i, n_full, tm, last_rows, body)


def _mm_res(a, w, xp, xs, xs_block, mod, gate_off, *, n_p, n_s, nb_p, nb_s, tm, tn):
    k = a.shape[1]
    n = w.shape[1]
    n_full = n_p // tm
    kern = functools.partial(
        _mm_res_kernel, n_full=n_full, tm=tm, last_rows=n_s,
        tiles_per_b=(n_p // nb_p) // tm, nb_p=nb_p, nb_s=nb_s, s_len=n_s // nb_s)
    return pl.pallas_call(
        kern,
        grid=(n_full + 1, n // tn),
        in_specs=[pl.BlockSpec((tm, k), lambda i, j: (i, 0)),
                  pl.BlockSpec((k, tn), lambda i, j: (0, j)),
                  pl.BlockSpec((tm, tn), lambda i, j: (jnp.minimum(i, n_full - 1), j)),
                  pl.BlockSpec((n_s, tn), lambda i, j: (xs_block, j)),
                  pl.BlockSpec((MOD_ROWS, tn), lambda i, j: (0, gate_off // tn + j))],
        out_specs=pl.BlockSpec((tm, tn), lambda i, j: (i, j)),
        out_shape=jax.ShapeDtypeStruct((n_p + n_s, n), F32),
        compiler_params=_cparams(2),
        name="mm_res",
    )(a, w, xp, xs, mod)


def _swiglu_kernel(a_ref, wg_ref, wu_ref, o_ref, *, n_full, tm, last_rows):
    i = pl.program_id(0)
    wg = wg_ref[...].astype(BF16)
    wu = wu_ref[...].astype(BF16)

    def body(rows, _):
        a = a_ref[rows, :]
        g = _dot(a, wg)
        u = _dot(a, wu)
        o_ref[rows, :] = (g * jax.nn.sigmoid(g) * u).astype(o_ref.dtype)

    _row_tiles(i, n_full, tm, last_rows, body)


def _swiglu(a, wg, wu, *, n_p, n_s, tm, tn):
    t, k = a.shape
    n = wg.shape[1]
    n_full = n_p // tm
    kern = functools.partial(_swiglu_kernel, n_full=n_full, tm=tm, last_rows=n_s)
    return pl.pallas_call(
        kern,
        grid=(n_full + 1, pl.cdiv(n, tn)),
        in_specs=[pl.BlockSpec((tm, k), lambda i, j: (i, 0), pipeline_mode=pl.Buffered(1)),
                  pl.BlockSpec((k, tn), lambda i, j: (0, j)),
                  pl.BlockSpec((k, tn), lambda i, j: (0, j))],
        out_specs=pl.BlockSpec((tm, tn), lambda i, j: (i, j)),
        out_shape=jax.ShapeDtypeStruct((t, n), BF16),
        compiler_params=_cparams(2),
        name="swiglu_up",
    )(a, wg, wu)


def _down(a, w, x_all, mod, gate_off, *, n_p, n_s, nb_p, nb_s, tm, tn):
    k = a.shape[1]
    n = w.shape[1]
    n_full = n_p // tm
    kern = functools.partial(
        _mm_res_kernel, n_full=n_full, tm=tm, last_rows=n_s,
        tiles_per_b=(n_p // nb_p) // tm, nb_p=nb_p, nb_s=nb_s, s_len=n_s // nb_s)
    return pl.pallas_call(
        kern,
        grid=(n_full + 1, n // tn),
        in_specs=[pl.BlockSpec((tm, k), lambda i, j: (i, 0), pipeline_mode=pl.Buffered(1)),
                  pl.BlockSpec((k, tn), lambda i, j: (0, j)),
                  pl.BlockSpec((tm, tn), lambda i, j: (jnp.minimum(i, n_full - 1), j)),
                  pl.BlockSpec((n_s, tn), lambda i, j: (n_p // n_s, j)),
                  pl.BlockSpec((MOD_ROWS, tn), lambda i, j: (0, gate_off // tn + j))],
        out_specs=pl.BlockSpec((tm, tn), lambda i, j: (i, j)),
        out_shape=jax.ShapeDtypeStruct((n_p + n_s, n), F32),
        compiler_params=_cparams(2),
        name="ffn_down",
    )(a, w, x_all, x_all, mod)


def _final_norm_kernel(x_ref, g_ref, o_ref):
    o_ref[...] = _rms(x_ref[...], g_ref[...])


def _final_norm(x_all, g, *, tr, row_block0, n_tiles):
    d = x_all.shape[1]
    return pl.pallas_call(
        _final_norm_kernel,
        grid=(n_tiles,),
        in_specs=[pl.BlockSpec((tr, d), lambda i: (row_block0 + i, 0)),
                  pl.BlockSpec((1, d), lambda i: (0, 0))],
        out_specs=pl.BlockSpec((tr, d), lambda i: (i, 0)),
        out_shape=jax.ShapeDtypeStruct((n_tiles * tr, d), F32),
        compiler_params=_cparams(1),
        name="final_norm",
    )(x_all, g.reshape(1, d))


def _rope_tables(pos):
    half = MLA_ROPE // 2
    inv = 1.0 / (ROPE_THETA ** (jnp.arange(half, dtype=F32) / half))
    ang = pos.astype(F32)[:, None] * inv[None, :]
    cos, sin = jnp.cos(ang), jnp.sin(ang)
    return jnp.concatenate([cos, cos], axis=-1), jnp.concatenate([sin, sin], axis=-1)


def _largest_tile(n, cap, unit=LANE):
    best = unit
    for m in range(unit, cap + 1, unit):
        if n % m == 0:
            best = m
    return best


def kernel(x_prompt, x_sample, cache_mla_latent, cache_mla_krope, cache_sb_k, cache_sb_v, c_prompt, c_sample, w_ada, b_ada, g_mix, g_ffn, w_in, g_q_lat, g_kv_lat, w_uq, w_uk, w_uv, g_out_mla, g_out_sb, w_out, w_gate, w_up, w_down, g_final):
    depth = w_in.shape[0]
    assert depth == 1, "single-layer stack"
    nb_p, s_p, d = x_prompt.shape
    nb_s, s_s, _ = x_sample.shape
    past = cache_mla_latent.shape[2]
    d_ff = w_gate.shape[2]
    n_p, n_s = nb_p * s_p, nb_s * s_s
    assert nb_p + nb_s <= MOD_ROWS and n_p % n_s == 0

    tm = min(1024, s_p)
    tr = min(512, s_p)
    tq_mla = min(512, s_p)
    tq_sb = min(512, s_p)
    tk_sb = min(256, s_p)
    assert s_p % tm == 0 and s_p % tr == 0 and n_p % tm == 0

    xp = x_prompt.reshape(n_p, d)
    xs = x_sample.reshape(n_s, d)

    c_all = jnp.concatenate(
        [c_prompt, c_sample, jnp.zeros((MOD_ROWS - nb_p - nb_s, d), F32)], axis=0)
    mod = _adaln(c_all, w_ada[0], b_ada[0])
    sh_m, sc_m, gt_m, sh_f, sc_f, gt_f = (k * d for k in range(N_MOD))

    pos_all = jnp.concatenate([jnp.tile(jnp.arange(s_p), nb_p),
                               jnp.tile(past + jnp.arange(s_s), nb_s)])
    cos2, sin2 = _rope_tables(pos_all)
    half = MLA_ROPE // 2
    cos_t, sin_t = cos2[:n_p, :half].T, sin2[:n_p, :half].T

    h = _norm_mod(xp, xs, 0, mod, g_mix[0], n_p=n_p, n_s=n_s, nb_p=nb_p, nb_s=nb_s,
                  sh_off=sh_m, sc_off=sc_m, tr=tr)
    w_in0 = w_in[0]
    na = -(-MLA_IN // LANE) * LANE
    w_a = w_in0[:, :na].astype(BF16)
    w_sb = w_in0[:, MLA_IN:].astype(BF16)
    qn, lat, kr = _inproj_a(h, w_a, g_q_lat[0], g_kv_lat[0], cos2, sin2, n_p=n_p, n_s=n_s, tm=tr)

    tn_sb = 512
    nsb = SB_WIDTH // tn_sb
    nft = n_p // tm
    sbq = _mm(h, w_sb, tm=tm, row_block0=0, n_full=nft, last_rows=n_s, tn=tn_sb, col_block0=0,
              n_cols=SB_WIDTH, out_dtype=BF16, scale=SB_Q_SCALE, name="sb_q")
    p_sbk = _mm(h, w_sb, tm=tm, row_block0=0, n_full=nft, last_rows=0, tn=tn_sb,
                col_block0=nsb, n_cols=SB_WIDTH, out_dtype=F32, name="sb_k_prompt")
    p_sbv = _mm(h, w_sb, tm=tm, row_block0=0, n_full=nft, last_rows=0, tn=tn_sb,
                col_block0=2 * nsb, n_cols=SB_WIDTH, out_dtype=F32, name="sb_v_prompt")
    s_sbk = _mm(h, w_sb, tm=n_s, row_block0=n_p // n_s, n_full=1, last_rows=0, tn=tn_sb,
                col_block0=nsb, n_cols=SB_WIDTH, out_dtype=F32, name="sb_k_sample")
    s_sbv = _mm(h, w_sb, tm=n_s, row_block0=n_p // n_s, n_full=1, last_rows=0, tn=tn_sb,
                col_block0=2 * nsb, n_cols=SB_WIDTH, out_dtype=F32, name="sb_v_sample")

    w_uq_ht = jnp.transpose(w_uq[0], (1, 2, 0)).astype(BF16)
    qt_p, q_s = _qheads(qn, w_uq_ht, cos_t, sin_t, cos2, sin2, n_p=n_p, n_s=n_s, tm=tm)
    wk = w_uk[0].reshape(KV_LORA, MLA_HEADS * MLA_NOPE).astype(BF16)
    wv = w_uv[0].reshape(KV_LORA, MLA_WIDTH).astype(BF16)
    kcat, vt_p = _kv_up(lat, kr, wk, wv.T, n_p=n_p, tk=tq_mla)
    op_mla = _mla_prompt(qt_p, kcat, vt_p, nb=nb_p, s_len=s_p, tq=tq_mla, hp=2)

    w_uk_t = jnp.transpose(w_uk[0], (1, 2, 0)).astype(BF16)
    qabs = _qabs(q_s, w_uk_t, n_s=n_s)
    os_mla = _mla_sample(qabs, q_s, cache_mla_latent[0], cache_mla_krope[0], lat, kr, wv,
                         n_p=n_p, nb=nb_s, s_len=s_s)

    op_sb = _sb_prompt(sbq, p_sbk, p_sbv, nb=nb_p, s_len=s_p, tq=tq_sb, tk=tk_sb)
    os_sb = _sb_sample(sbq, s_sbk, s_sbv, cache_sb_k[0], cache_sb_v[0], n_p=n_p, nb=nb_s,
                       s_len=s_s, tk=min(512, past))

    merged = _merge(op_mla, op_sb, os_mla, os_sb, g_out_mla[0], g_out_sb[0],
                    n_p=n_p, n_s=n_s, tr=tr)
    x1 = _mm_res(merged, w_out[0], xp, xs, 0, mod, gt_m, n_p=n_p, n_s=n_s, nb_p=nb_p,
                 nb_s=nb_s, tm=tm, tn=min(512, d))

    h2 = _norm_mod(x1, x1, n_p // n_s, mod, g_ffn[0], n_p=n_p, n_s=n_s, nb_p=nb_p, nb_s=nb_s,
                   sh_off=sh_f, sc_off=sc_f, tr=tr)
    act = _swiglu(h2, w_gate[0], w_up[0], n_p=n_p, n_s=n_s, tm=min(2048, s_p),
                  tn=_largest_tile(d_ff, 256))
    x2 = _down(act, w_down[0].astype(BF16), x1, mod, gt_f, n_p=n_p, n_s=n_s, nb_p=nb_p,
               nb_s=nb_s, tm=tm, tn=min(256, d))

    y_p = _final_norm(x2, g_final, tr=tr, row_block0=0, n_tiles=n_p // tr)
    y_s = _final_norm(x2, g_final, tr=n_s, row_block0=n_p // n_s, n_tiles=1)

    return (y_p.reshape(nb_p, s_p, d), y_s.reshape(nb_s, s_s, d),
            lat[:n_p].reshape(1, nb_p, s_p, KV_LORA),
            kr[:n_p].reshape(1, nb_p, s_p, MLA_ROPE),
            p_sbk.reshape(1, nb_p, s_p, SB_HEADS, SB_DIM),
            p_sbv.reshape(1, nb_p, s_p, SB_HEADS, SB_DIM),
            lat[n_p:].reshape(1, nb_s, s_s, KV_LORA),
            kr[n_p:].reshape(1, nb_s, s_s, MLA_ROPE),
            s_sbk.reshape(1, nb_s, s_s, SB_HEADS, SB_DIM),
            s_sbv.reshape(1, nb_s, s_s, SB_HEADS, SB_DIM))
```
